```python
import math
import jax, jax.numpy as jnp
from jax import lax
import numpy as np

D_MODEL = 1024
BATCH = 16
SEQ = 2048
DEPTH = 1

DN_HEADS = 4
DN_DK = 128
DN_DV = 128
CONV_K = 4
CHUNK = 64
DSA_HEADS = 8
DSA_DH = 64
Q_RANK = 256
KV_RANK = 128
IDX_HEADS = 8
IDX_DIM = 64
TOPK_MAX = 256
Q_BLOCK = 128
NUM_BUCKETS = 32
MAX_DISTANCE = 128
MEM_LEN = 256
X_HEADS = 4
X_DH = D_MODEL // X_HEADS
D_FF = -(-8 * D_MODEL // (3 * 256)) * 256
EPS = 1e-6
DN_QK_W = DN_HEADS * DN_DK
DN_V_W = DN_HEADS * DN_DV
PROJ_SIZES = (DN_QK_W, DN_QK_W, DN_V_W, DN_V_W, DN_HEADS, DN_HEADS, Q_RANK, KV_RANK, IDX_DIM, IDX_HEADS)
PROJ_OUT = sum(PROJ_SIZES)
MIX_WIDTH = DN_V_W + DSA_HEADS * DSA_DH

kernel_name = 'hybrid_gdn_dsa_block'


def rmsnorm(x, g):
    xf = x.astype(jnp.float32)
    y = xf * lax.rsqrt(jnp.mean(xf * xf, axis=-1, keepdims=True) + EPS)
    return (y * g.astype(jnp.float32)).astype(x.dtype)


def layernorm(x, g, b):
    xf = x.astype(jnp.float32)
    mu = jnp.mean(xf, axis=-1, keepdims=True)
    xc = xf - mu
    y = xc * lax.rsqrt(jnp.mean(xc * xc, axis=-1, keepdims=True) + EPS)
    return (y * g.astype(jnp.float32) + b.astype(jnp.float32)).astype(x.dtype)


def l2norm(x):
    xf = x.astype(jnp.float32)
    return (xf * lax.rsqrt(jnp.sum(xf * xf, axis=-1, keepdims=True) + EPS)).astype(x.dtype)


def causal_depthwise_conv(x, w):
    width = w.shape[0]
    return lax.conv_general_dilated(
        x, w[:, None, :].astype(x.dtype), window_strides=(1,), padding=[(width - 1, 0)],
        dimension_numbers=('NWC', 'WIO', 'NWC'), feature_group_count=x.shape[-1])


def relative_bucket(dist):
    max_exact = NUM_BUCKETS // 2
    d = jnp.maximum(dist, 0)
    log_ratio = jnp.log(jnp.maximum(d, max_exact).astype(jnp.float32) / max_exact) / math.log(MAX_DISTANCE / max_exact)
    large = max_exact + (log_ratio * (NUM_BUCKETS - max_exact)).astype(jnp.int32)
    return jnp.where(d < max_exact, d, jnp.minimum(large, NUM_BUCKETS - 1))


def gated_delta_rule_chunked(q, k, v, beta, g):
    b, s, nh, dk = q.shape
    dv = v.shape[-1]
    nc = s // CHUNK
    f32 = jnp.float32

    def chunks(t):
        return jnp.moveaxis(t.astype(f32).reshape(b, nc, CHUNK, nh, *t.shape[3:]), 3, 1)

    q, k, v, beta, g = chunks(q), chunks(k), chunks(v), chunks(beta), chunks(g)
    g_cum = jnp.cumsum(g, axis=-1)
    pos = jnp.arange(CHUNK)
    incl = pos[:, None] >= pos[None, :]
    strict = pos[:, None] > pos[None, :]
    decay = jnp.exp(jnp.where(incl, g_cum[..., :, None] - g_cum[..., None, :], -jnp.inf))
    kk = jnp.einsum('bhncd,bhnmd->bhncm', k, k)
    a_mat = jnp.where(strict, beta[..., :, None] * kk * decay, 0.0) + jnp.eye(CHUNK, dtype=f32)
    rhs = jnp.concatenate([v * beta[..., None], k * (beta * jnp.exp(g_cum))[..., None]], axis=-1)
    sol = lax.linalg.triangular_solve(a_mat, rhs, left_side=True, lower=True, unit_diagonal=True)
    u, w = sol[..., :dv], sol[..., dv:]
    qk = jnp.where(incl, jnp.einsum('bhncd,bhnmd->bhncm', q, k) * decay, 0.0)
    q_dec = q * jnp.exp(g_cum)[..., None]
    k_dec = k * jnp.exp(g_cum[..., -1:] - g_cum)[..., None]
    last = jnp.exp(g_cum[..., -1])

    def step(state, xs):
        u_c, w_c, qk_c, qd_c, kd_c, last_c = xs
        v_new = u_c - jnp.einsum('bhcd,bhde->bhce', w_c, state)
        o_c = jnp.einsum('bhcd,bhde->bhce', qd_c, state) + jnp.einsum('bhcm,bhme->bhce', qk_c, v_new)
        state = state * last_c[..., None, None] + jnp.einsum('bhcd,bhce->bhde', kd_c, v_new)
        return state, o_c

    xs = tuple(jnp.moveaxis(t, 2, 0) for t in (u, w, qk, q_dec, k_dec, last))
    _, o = lax.scan(step, jnp.zeros((b, nh, dk, dv), f32), xs)
    return jnp.transpose(o, (1, 0, 3, 2, 4)).reshape(b, s, nh, dv)


def dsa_sparse_attention(q_abs, c_kv, q_idx, w_idx, k_idx, rel_bias):
    b, s = c_kv.shape[:2]
    k_sel = min(TOPK_MAX, s // 4)
    nb = s // Q_BLOCK
    key_pos = jnp.arange(s)

    def blocks(t):
        return jnp.moveaxis(t.reshape(b, nb, Q_BLOCK, *t.shape[2:]), 1, 0)

    def one_block(args):
        qi, wi, qa, tq = args
        rel = jax.nn.relu(jnp.einsum('bqhd,bsd->bqhs', qi, k_idx))
        score = jnp.einsum('bqh,bqhs->bqs', wi, rel).astype(jnp.float32)
        score = jnp.where(key_pos[None, None, :] <= tq[None, :, None], score, -jnp.inf)
        _, idx = lax.top_k(score, k_sel)
        c_sel = jax.vmap(lambda c, i: c[i])(c_kv, idx)
        dist = tq[None, :, None] - idx
        bias = jnp.moveaxis(rel_bias[relative_bucket(dist)], -1, 2)
        logits = jnp.einsum('bqhc,bqkc->bqhk', qa, c_sel).astype(jnp.float32) * (DSA_DH ** -0.5)
        logits = jnp.where((dist >= 0)[:, :, None, :], logits + bias.astype(jnp.float32), -1e30)
        p = jax.nn.softmax(logits, axis=-1).astype(c_sel.dtype)
        return jnp.einsum('bqhk,bqkc->bqhc', p, c_sel)

    t_blocks = jnp.arange(s).reshape(nb, Q_BLOCK)
    o = lax.map(one_block, (blocks(q_idx), blocks(w_idx), blocks(q_abs), t_blocks))
    return jnp.moveaxis(o, 0, 1).reshape(b, s, *q_abs.shape[2:])


def setup_inputs(seed: int = 0) -> dict:
    key = jax.random.key(seed)
    ks = jax.random.split(key, 32)
    L = DEPTH
    f32 = jnp.float32

    def nrm(k, shape, fan_in):
        return jax.random.normal(k, shape, f32) * (fan_in ** -0.5)

    def gain(k, shape):
        return 1.0 + 0.02 * jax.random.normal(k, shape, f32)

    dt = jnp.exp(jax.random.uniform(ks[5], (L, DN_HEADS), f32, math.log(1e-3), math.log(1e-1)))
    return {
        'x': jax.random.normal(ks[0], (BATCH, SEQ, D_MODEL), f32),
        'mem': jax.random.normal(ks[1], (BATCH, MEM_LEN, D_MODEL), f32),
        'g_mix': gain(ks[2], (L, D_MODEL)),
        'w_in': nrm(ks[3], (L, D_MODEL, PROJ_OUT), D_MODEL),
        'conv_w': nrm(ks[4], (L, CONV_K, 2 * DN_QK_W + DN_V_W), CONV_K),
        'a_log': jnp.log(jax.random.uniform(ks[6], (L, DN_HEADS), f32, 1.0, 16.0)),
        'dt_bias': dt + jnp.log(-jnp.expm1(-dt)),
        'dn_norm_g': gain(ks[7], (L, DN_DV)),
        'q_norm_g': gain(ks[8], (L, Q_RANK)),
        'kv_norm_g': gain(ks[9], (L, KV_RANK)),
        'w_uq': nrm(ks[10], (L, Q_RANK, DSA_HEADS * DSA_DH), Q_RANK),
        'w_uk': nrm(ks[11], (L, DSA_HEADS, DSA_DH, KV_RANK), DSA_DH),
        'w_uv': nrm(ks[12], (L, DSA_HEADS, KV_RANK, DSA_DH), KV_RANK),
        'w_qidx': nrm(ks[13], (L, Q_RANK, IDX_HEADS * IDX_DIM), Q_RANK),
        'kidx_ln_g': gain(ks[14], (L, IDX_DIM)),
        'kidx_ln_b': 0.02 * jax.random.normal(ks[15], (L, IDX_DIM), f32),
        'rel_bias': 0.5 * jax.random.normal(ks[16], (NUM_BUCKETS, DSA_HEADS), f32),
        'w_out': nrm(ks[17], (L, MIX_WIDTH, D_MODEL), MIX_WIDTH),
        'g_xattn': gain(ks[18], (L, D_MODEL)),
        'g_mem': gain(ks[19], (L, D_MODEL)),
        'w_xq': nrm(ks[20], (L, D_MODEL, D_MODEL), D_MODEL),
        'w_xkv': nrm(ks[21], (L, D_MODEL, 2 * D_MODEL), D_MODEL),
        'w_xo': nrm(ks[22], (L, D_MODEL, D_MODEL), D_MODEL),
        'g_ffn': gain(ks[23], (L, D_MODEL)),
        'w_gate': nrm(ks[24], (L, D_MODEL, D_FF), D_MODEL),
        'w_up': nrm(ks[25], (L, D_MODEL, D_FF), D_MODEL),
        'w_down': nrm(ks[26], (L, D_FF, D_MODEL), D_FF),
        'g_final': gain(ks[27], (D_MODEL,)),
    }


def reference(x, mem, g_mix, w_in, conv_w, a_log, dt_bias, dn_norm_g, q_norm_g, kv_norm_g, w_uq, w_uk, w_uv,
              w_qidx, kidx_ln_g, kidx_ln_b, rel_bias, w_out, g_xattn, g_mem, w_xq, w_xkv, w_xo, g_ffn,
              w_gate, w_up, w_down, g_final):
    b, s, _ = x.shape
    m_len = mem.shape[1]
    f32 = jnp.float32
    offs = np.cumsum(PROJ_SIZES)[:-1].tolist()
    for l in range(DEPTH):
        h = rmsnorm(x, g_mix[l])
        proj = h @ w_in[l]
        dq, dk, dv, dz, db, da, cq, ckv, kix, wix = jnp.split(proj, offs, axis=-1)

        qkv = jax.nn.silu(causal_depthwise_conv(jnp.concatenate([dq, dk, dv], axis=-1), conv_w[l]))
        dq, dk, dv = jnp.split(qkv, [DN_QK_W, 2 * DN_QK_W], axis=-1)
        q = l2norm(dq.reshape(b, s, DN_HEADS, DN_DK)) * (DN_DK ** -0.5)
        k = l2norm(dk.reshape(b, s, DN_HEADS, DN_DK))
        v = dv.reshape(b, s, DN_HEADS, DN_DV)
        beta = jax.nn.sigmoid(db.astype(f32))
        g = -jnp.exp(a_log[l].astype(f32)) * jax.nn.softplus(da.astype(f32) + dt_bias[l].astype(f32))
        o_dn = gated_delta_rule_chunked(q, k, v, beta, g).astype(x.dtype)
        o_dn = rmsnorm(o_dn, dn_norm_g[l]) * jax.nn.silu(dz).reshape(b, s, DN_HEADS, DN_DV)
        o_dn = o_dn.reshape(b, s, DN_V_W)

        cq = rmsnorm(cq, q_norm_g[l])
        q_d = (cq @ w_uq[l]).reshape(b, s, DSA_HEADS, DSA_DH)
        q_abs = jnp.einsum('bshd,hdc->bshc', q_d, w_uk[l])
        c_kv = rmsnorm(ckv, kv_norm_g[l])
        q_idx = (cq @ w_qidx[l]).reshape(b, s, IDX_HEADS, IDX_DIM)
        k_idx = layernorm(kix, kidx_ln_g[l], kidx_ln_b[l])
        w_idx = wix * (IDX_HEADS ** -0.5 * IDX_DIM ** -0.5)
        o_lat = dsa_sparse_attention(q_abs, c_kv, q_idx, w_idx, k_idx, rel_bias)
        o_dsa = jnp.einsum('bshc,hcd->bshd', o_lat, w_uv[l]).reshape(b, s, DSA_HEADS * DSA_DH)

        x = x + jnp.concatenate([o_dn, o_dsa], axis=-1) @ w_out[l]

        hx = rmsnorm(x, g_xattn[l])
        mn = rmsnorm(mem, g_mem[l])
        qx = (hx @ w_xq[l]).reshape(b, s, X_HEADS, X_DH)
        kvx = (mn @ w_xkv[l]).reshape(b, m_len, 2, X_HEADS, X_DH)
        logits = jnp.einsum('bshd,bmhd->bhsm', qx, kvx[:, :, 0]).astype(f32) * (X_DH ** -0.5)
        p = jax.nn.softmax(logits, axis=-1).astype(x.dtype)
        ox = jnp.einsum('bhsm,bmhd->bshd', p, kvx[:, :, 1]).reshape(b, s, D_MODEL)
        x = x + ox @ w_xo[l]

        hf = rmsnorm(x, g_ffn[l])
        x = x + (jax.nn.silu(hf @ w_gate[l]) * (hf @ w_up[l])) @ w_down[l]
    return rmsnorm(x, g_final)
```

```python
import functools
import math

import jax
import jax.numpy as jnp
import numpy as np
from jax import lax
from jax.experimental import pallas as pl
from jax.experimental.pallas import tpu as pltpu

F32 = jnp.float32
BF16 = jnp.bfloat16
I32 = jnp.int32

EPS = 1e-6
DN_HEADS = 4
DN_D = 128
CONV_K = 4
DSA_HEADS = 8
DSA_DH = 64
Q_RANK = 256
KV_RANK = 128
IDX_HEADS = 8
IDX_DIM = 64
TOPK_MAX = 256
NUM_BUCKETS = 32
MAX_DISTANCE = 128
X_HEADS = 4

SM_KIX = 0
SM_WIX = 64
SM_DB = 72
SM_DA = 76
SM_W = 128

LANE = 128
GDN_CHUNK = 128
DSA_TQ = 256
DSA_KB = 256
TOK_TILE = 512
VMEM_LIMIT = 56 * 1024 * 1024

INT_MIN = -(2 ** 31)
NEG_MASK = -1e30
NEG_INIT = -1e29


def _dot(a, b):
    return jnp.dot(a, b, preferred_element_type=F32)


def _dot_nt(a, b):
    return lax.dot_general(a, b, (((1,), (1,)), ((), ())), preferred_element_type=F32)


def _dot_tn(a, b):
    return lax.dot_general(a, b, (((0,), (0,)), ((), ())), preferred_element_type=F32)


def _split2(a):
    hi = a.astype(BF16)
    lo = (a - hi.astype(F32)).astype(BF16)
    return hi, lo


def _split3(a):
    hi = a.astype(BF16)
    r = a - hi.astype(F32)
    mid = r.astype(BF16)
    lo = (r - mid.astype(F32)).astype(BF16)
    return hi, mid, lo


def _dot3(a, b, dot=_dot):
    ah, al = _split2(a)
    bh, bl = _split2(b)
    return dot(ah, bh) + (dot(ah, bl) + dot(al, bh))


def _dot_exact_rhs(sel_bf16, b, dot=_dot):
    bh, bm, bl = _split3(b)
    return dot(sel_bf16, bh) + (dot(sel_bf16, bm) + dot(sel_bf16, bl))


def _dot_exact_lhs(a, sel_bf16, dot=_dot):
    ah, am, al = _split3(a)
    return dot(ah, sel_bf16) + (dot(am, sel_bf16) + dot(al, sel_bf16))


def _rms(x):
    return x * lax.rsqrt(jnp.mean(x * x, axis=-1, keepdims=True) + EPS)


def _sigmoid(x):
    return 1.0 / (1.0 + jnp.exp(-x))


def _inproj_kernel(x_ref, gmix_ref, w1_ref, wwt_ref, qng_ref, kvg_ref, lng_ref, lnb_ref,
                   wuqt_ref, wukt_ref, wqit_ref, eye_ref,
                   qkv_ref, dz_ref, small_ref, kidx_ref, widxt_ref, ckv_ref, ckvt_ref,
                   qidxt_ref, qabst_ref):
    n_qkv = 3 * DN_HEADS * DN_D
    n_z = DN_HEADS * DN_D
    o_cq = n_qkv + n_z
    o_ckv = o_cq + Q_RANK
    o_sm = o_ckv + KV_RANK
    x = x_ref[...]
    hb = (_rms(x) * gmix_ref[...]).astype(BF16)
    qkv_ref[...] = _dot(hb, w1_ref[:, 0:n_qkv])
    dz_ref[...] = _dot(hb, w1_ref[:, n_qkv:o_cq])
    small = _dot(hb, w1_ref[:, o_sm:o_sm + SM_W])
    small_ref[...] = small
    lane = lax.broadcasted_iota(I32, small.shape, 1)
    is_k = lane < IDX_DIM
    mu = jnp.sum(jnp.where(is_k, small, 0.0), axis=-1, keepdims=True) * (1.0 / IDX_DIM)
    xc = jnp.where(is_k, small - mu, 0.0)
    var = jnp.sum(xc * xc, axis=-1, keepdims=True) * (1.0 / IDX_DIM)
    kn = xc * lax.rsqrt(var + EPS) * lng_ref[...] + lnb_ref[...]
    kidx_ref[...] = kn[:, :IDX_DIM].astype(BF16)
    widxt_ref[...] = _dot_nt(wwt_ref[...], hb) * (IDX_HEADS ** -0.5 * IDX_DIM ** -0.5)
    cq = _dot(hb, w1_ref[:, o_cq:o_ckv])
    cqn = (_rms(cq) * qng_ref[...]).astype(BF16)
    qdt = _dot_nt(wuqt_ref[...], cqn).astype(BF16)
    for h in range(DSA_HEADS):
        qa = _dot(wukt_ref[h], qdt[h * DSA_DH:(h + 1) * DSA_DH, :])
        qabst_ref[h] = (qa * (DSA_DH ** -0.5)).astype(BF16)
    qidxt_ref[...] = _dot_nt(wqit_ref[...], cqn).astype(BF16)
    ckv = _dot(hb, w1_ref[:, o_ckv:o_sm])
    c = (_rms(ckv) * kvg_ref[...]).astype(BF16)
    ckv_ref[...] = c
    ct = _dot_nt(eye_ref[...], c).astype(BF16)
    for t in range(ckvt_ref.shape[0]):
        ckvt_ref[t] = ct[:, t * DSA_KB:(t + 1) * DSA_KB]


def _inproj(x, gmix, w1, wwt, qng, kvg, lng, lnb, wuqt, wukt, wqit, eye):
    b, s, d = x.shape
    tm = TOK_TILE
    nt = tm // DSA_KB
    full = lambda a: pl.BlockSpec(a.shape, lambda bi, i: (0,) * a.ndim)
    out_shape = (
        jax.ShapeDtypeStruct((b, s, 3 * DN_HEADS * DN_D), F32),
        jax.ShapeDtypeStruct((b, s, DN_HEADS * DN_D), F32),
        jax.ShapeDtypeStruct((b, s, SM_W), F32),
        jax.ShapeDtypeStruct((b, s, IDX_DIM), BF16),
        jax.ShapeDtypeStruct((b, IDX_HEADS, s), F32),
        jax.ShapeDtypeStruct((b, s, KV_RANK), BF16),
        jax.ShapeDtypeStruct((b, s // DSA_KB, KV_RANK, DSA_KB), BF16),
        jax.ShapeDtypeStruct((b, IDX_HEADS * IDX_DIM, s), BF16),
        jax.ShapeDtypeStruct((b, DSA_HEADS, KV_RANK, s), BF16),
    )
    out_specs = (
        pl.BlockSpec((None, tm, 3 * DN_HEADS * DN_D), lambda bi, i: (bi, i, 0)),
        pl.BlockSpec((None, tm, DN_HEADS * DN_D), lambda bi, i: (bi, i, 0)),
        pl.BlockSpec((None, tm, SM_W), lambda bi, i: (bi, i, 0)),
        pl.BlockSpec((None, tm, IDX_DIM), lambda bi, i: (bi, i, 0)),
        pl.BlockSpec((None, IDX_HEADS, tm), lambda bi, i: (bi, 0, i)),
        pl.BlockSpec((None, tm, KV_RANK), lambda bi, i: (bi, i, 0)),
        pl.BlockSpec((None, nt, KV_RANK, DSA_KB), lambda bi, i: (bi, i, 0, 0)),
        pl.BlockSpec((None, IDX_HEADS * IDX_DIM, tm), lambda bi, i: (bi, 0, i)),
        pl.BlockSpec((None, DSA_HEADS, KV_RANK, tm), lambda bi, i: (bi, 0, 0, i)),
    )
    return pl.pallas_call(
        _inproj_kernel,
        grid=(b, s // tm),
        in_specs=[pl.BlockSpec((None, tm, d), lambda bi, i: (bi, i, 0)),
                  full(gmix), full(w1), full(wwt), full(qng), full(kvg), full(lng), full(lnb),
                  full(wuqt), full(wukt), full(wqit), full(eye)],
        out_specs=out_specs,
        out_shape=out_shape,
        compiler_params=pltpu.CompilerParams(
            dimension_semantics=("parallel", "parallel"), vmem_limit_bytes=VMEM_LIMIT),
        name="inproj",
    )(x, gmix, w1, wwt, qng, kvg, lng, lnb, wuqt, wukt, wqit, eye)


def _gdn_kernel(qp_ref, kp_ref, vp_ref, cwq_ref, cwk_ref, cwv_ref, small_ref, alog_ref, dtb_ref,
                dz_ref, gn_ref, o_ref, xp_ref, state_ref):
    h = pl.program_id(1)
    s = qp_ref.shape[0]
    c_sz = GDN_CHUNK
    pad = 8
    for a, src in enumerate((qp_ref, kp_ref, vp_ref)):
        xp_ref[a, 0:pad, :] = jnp.zeros((pad, DN_D), F32)
        xp_ref[a, pad:pad + s, :] = src[...]
    state_ref[...] = jnp.zeros_like(state_ref)

    row = lax.broadcasted_iota(I32, (c_sz, c_sz), 0)
    col = lax.broadcasted_iota(I32, (c_sz, c_sz), 1)
    incl = row >= col
    strict = row > col
    tril = jnp.where(incl, 1.0, 0.0).astype(BF16)
    eye_f = jnp.where(row == col, 1.0, 0.0)
    sel_db = jnp.where(row == SM_DB + h, 1.0, 0.0).astype(BF16)
    sel_da = jnp.where(row == SM_DA + h, 1.0, 0.0).astype(BF16)
    sel_da_t = jnp.where(col == SM_DA + h, 1.0, 0.0).astype(BF16)
    neg_a = -jnp.exp(alog_ref[...])
    dtb = dtb_ref[...]
    cws = (cwq_ref, cwk_ref, cwv_ref)

    def conv_silu(a, base):
        acc = None
        for k in range(CONV_K):
            xs = xp_ref[a, pl.ds(base + pad - (CONV_K - 1) + k, c_sz), :]
            term = xs * cws[a][k:k + 1, :]
            acc = term if acc is None else acc + term
        return acc * _sigmoid(acc)

    def chunk(ci, carry):
        base = pl.multiple_of(ci * c_sz, c_sz)
        qc = conv_silu(0, base)
        kc = conv_silu(1, base)
        v = conv_silu(2, base)
        q = qc * lax.rsqrt(jnp.sum(qc * qc, axis=-1, keepdims=True) + EPS) * (DN_D ** -0.5)
        k = kc * lax.rsqrt(jnp.sum(kc * kc, axis=-1, keepdims=True) + EPS)
        sm = small_ref[pl.ds(base, c_sz), :]
        beta_all = _sigmoid(sm)
        z = sm + dtb
        g_all = neg_a * (jnp.maximum(z, 0.0) + jnp.log(1.0 + jnp.exp(-jnp.abs(z))))
        gc_all = _dot_exact_rhs(tril, g_all)
        beta_b = _dot_exact_lhs(beta_all, sel_db)
        gc_b = _dot_exact_lhs(gc_all, sel_da)
        gc_r = _dot_exact_rhs(sel_da_t, gc_all, dot=_dot_nt)
        decay = jnp.where(incl, jnp.exp(jnp.where(incl, gc_b - gc_r, 0.0)), 0.0)
        kk = _dot3(k, k, dot=_dot_nt)
        qk = _dot3(q, k, dot=_dot_nt)
        m = jnp.where(strict, -(beta_b * kk * decay), 0.0)
        t_inv = eye_f + m
        p = m
        for _ in range(int(math.log2(c_sz)) - 1):
            p = _dot3(p, p)
            t_inv = t_inv + _dot3(t_inv, p)
        e_gc = jnp.exp(gc_b)
        u = _dot3(t_inv, v * beta_b)
        w = _dot3(t_inv, k * (beta_b * e_gc))
        qkm = jnp.where(incl, qk * decay, 0.0)
        qd = q * e_gc
        g_last = gc_b[c_sz - 1:c_sz, :]
        kd = k * jnp.exp(g_last - gc_b)
        st = state_ref[...]
        v_new = u - _dot3(w, st)
        o = _dot3(qd, st) + _dot3(qkm, v_new)
        state_ref[...] = st * jnp.exp(g_last) + _dot3(kd, v_new, dot=_dot_tn)
        dz = dz_ref[pl.ds(base, c_sz), :]
        o_ref[pl.ds(base, c_sz), :] = (_rms(o) * gn_ref[...] * (dz * _sigmoid(dz))).astype(o_ref.dtype)
        return carry

    lax.fori_loop(0, s // c_sz, chunk, 0)


def _gdn(qkv, conv_w, small, alog_row, dtb_row, dz, gn):
    b, s, _ = qkv.shape
    nh = DN_HEADS
    col = lambda off: pl.BlockSpec((None, s, DN_D), lambda bi, h: (bi, 0, off + h))
    cw = lambda off: pl.BlockSpec((CONV_K, DN_D), lambda bi, h: (0, off + h))
    row = lambda a: pl.BlockSpec(a.shape, lambda bi, h: (0, 0))
    return pl.pallas_call(
        _gdn_kernel,
        grid=(b, nh),
        in_specs=[col(0), col(nh), col(2 * nh), cw(0), cw(nh), cw(2 * nh),
                  pl.BlockSpec((None, s, SM_W), lambda bi, h: (bi, 0, 0)),
                  row(alog_row), row(dtb_row),
                  pl.BlockSpec((None, s, DN_D), lambda bi, h: (bi, 0, h)),
                  row(gn)],
        out_specs=pl.BlockSpec((None, s, DN_D), lambda bi, h: (bi, 0, h)),
        out_shape=jax.ShapeDtypeStruct((b, s, nh * DN_D), BF16),
        scratch_shapes=[pltpu.VMEM((3, s + 8, DN_D), F32), pltpu.VMEM((DN_D, DN_D), F32)],
        compiler_params=pltpu.CompilerParams(
            dimension_semantics=("parallel", "parallel"), vmem_limit_bytes=VMEM_LIMIT),
        name="gdn",
    )(qkv, qkv, qkv, conv_w, conv_w, conv_w, small, alog_row, dtb_row, dz, gn)


def _dsa_kernel(bfar_ref, qidxt_ref, widxt_ref, kidx_ref, qabst_ref, ckv_ref, ckvt_ref, btab_ref,
                wuvt_ref, o_ref, keys_ref, thr_ref, acc_ref, m_ref, l_ref):
    i = pl.program_id(1)
    tq = DSA_TQ
    kb = DSA_KB
    s = kidx_ref.shape[0]
    k_sel = min(TOPK_MAX, s // 4)
    n_blk = i + 1
    srow = lax.broadcasted_iota(I32, (kb, tq), 0)
    tcol = lax.broadcasted_iota(I32, (kb, tq), 1) + i * tq

    def score_tile(j, carry):
        kb0 = pl.multiple_of(j * kb, kb)
        kblk = kidx_ref[pl.ds(kb0, kb), :]
        acc = jnp.zeros((kb, tq), F32)
        for h in range(IDX_HEADS):
            r = _dot(kblk, qidxt_ref[h * IDX_DIM:(h + 1) * IDX_DIM, :])
            acc = acc + widxt_ref[h:h + 1, :] * jnp.maximum(r, 0.0)
        bits = pltpu.bitcast(acc, I32)
        key = bits ^ ((bits >> 31) & jnp.int32(0x7FFFFFFF))
        keys_ref[pl.ds(kb0, kb), :] = jnp.where(srow + kb0 <= tcol, key, jnp.int32(INT_MIN))
        return carry

    lax.fori_loop(0, n_blk, score_tile, 0)

    def count(pred):
        def tile(j, c8):
            kb0 = pl.multiple_of(j * kb, kb)
            ind = jnp.where(pred(keys_ref[pl.ds(kb0, kb), :], kb0), 1, 0).astype(I32)
            return c8 + jnp.sum(ind.reshape(kb // 8, 8, tq), axis=0)
        c8 = lax.fori_loop(0, n_blk, tile, jnp.zeros((8, tq), I32))
        return jnp.sum(c8, axis=0, keepdims=True)

    thr_ref[...] = jnp.full((1, tq), INT_MIN + 1, I32)

    @pl.when(i * tq >= k_sel)
    def _search():
        c0 = count(lambda kt, kb0: kt >= 0)
        t0 = jnp.where(c0 >= k_sel, 0, INT_MIN).astype(I32)

        def bit_step(it, t):
            cand = t + jnp.left_shift(jnp.int32(1), 30 - it)
            c = count(lambda kt, kb0: kt >= cand)
            return jnp.where(c >= k_sel, cand, t)

        t = lax.fori_loop(0, 31, bit_step, t0)
        thr_ref[...] = t
        c_ge = count(lambda kt, kb0: kt >= t)
        c_gt = count(lambda kt, kb0: kt > t)

        @pl.when(jnp.max(c_ge) > k_sel)
        def _ties():
            need = k_sel - c_gt

            def pos_step(it, cut):
                cand = cut + jnp.left_shift(jnp.int32(1), (s.bit_length() - 2) - it)
                c = count(lambda kt, kb0: (kt == t) & (srow + kb0 < cand))
                return jnp.where(c < need, cand, cut)

            cut = lax.fori_loop(0, s.bit_length() - 1, pos_step, jnp.zeros((1, tq), I32))

            def demote(j, carry):
                kb0 = pl.multiple_of(j * kb, kb)
                kt = keys_ref[pl.ds(kb0, kb), :]
                keys_ref[pl.ds(kb0, kb), :] = jnp.where((kt == t) & (srow + kb0 > cut), t - 1, kt)
                return carry

            lax.fori_loop(0, n_blk, demote, 0)

    m_ref[...] = jnp.full(m_ref.shape, NEG_INIT, F32)
    l_ref[...] = jnp.zeros(l_ref.shape, F32)
    acc_ref[...] = jnp.zeros(acc_ref.shape, F32)
    thr = thr_ref[...]

    def attend(j, bias_of_head):
        kb0 = pl.multiple_of(j * kb, kb)
        sel = keys_ref[pl.ds(kb0, kb), :] >= thr
        cb = ckv_ref[pl.ds(kb0, kb), :]
        cbt = ckvt_ref[j]
        for h in range(DSA_HEADS):
            lg = _dot(cb, qabst_ref[h]) + bias_of_head(h)
            lg = jnp.where(sel, lg, NEG_MASK)
            m_old = m_ref[h]
            m_new = jnp.maximum(m_old, jnp.max(lg, axis=0, keepdims=True))
            alpha = jnp.exp(m_old - m_new)
            p = jnp.exp(lg - m_new)
            l_ref[h] = l_ref[h] * alpha + jnp.sum(p, axis=0, keepdims=True)
            acc_ref[h] = acc_ref[h] * alpha + _dot(cbt, p.astype(BF16))
            m_ref[h] = m_new

    def far_tile(j, carry):
        attend(j, lambda h: bfar_ref[h])
        return carry

    lax.fori_loop(0, jnp.maximum(i - 1, 0), far_tile, 0)

    @pl.when(i >= 1)
    def _sub_diagonal():
        attend(i - 1, lambda h: btab_ref[1, h])

    attend(i, lambda h: btab_ref[0, h])

    for h in range(DSA_HEADS):
        olat = (acc_ref[h] / l_ref[h]).astype(BF16)
        o_ref[h * DSA_DH:(h + 1) * DSA_DH, :] = _dot(wuvt_ref[h], olat).astype(o_ref.dtype)


def _dsa(bias_far, qidxt, widxt, kidx, qabst, ckv, ckvt, btab, wuvt):
    b, s, _ = kidx.shape
    tq = DSA_TQ
    full = lambda a: pl.BlockSpec(a.shape, lambda bi, i: (0,) * a.ndim)
    return pl.pallas_call(
        _dsa_kernel,
        grid=(b, s // tq),
        in_specs=[pl.BlockSpec(memory_space=pltpu.SMEM),
                  pl.BlockSpec((None, IDX_HEADS * IDX_DIM, tq), lambda bi, i: (bi, 0, i)),
                  pl.BlockSpec((None, IDX_HEADS, tq), lambda bi, i: (bi, 0, i)),
                  pl.BlockSpec((None, s, IDX_DIM), lambda bi, i: (bi, 0, 0)),
                  pl.BlockSpec((None, DSA_HEADS, KV_RANK, tq), lambda bi, i: (bi, 0, 0, i)),
                  pl.BlockSpec((None, s, KV_RANK), lambda bi, i: (bi, 0, 0)),
                  pl.BlockSpec((None, s // DSA_KB, KV_RANK, DSA_KB), lambda bi, i: (bi, 0, 0, 0)),
                  full(btab), full(wuvt)],
        out_specs=pl.BlockSpec((None, DSA_HEADS * DSA_DH, tq), lambda bi, i: (bi, 0, i)),
        out_shape=jax.ShapeDtypeStruct((b, DSA_HEADS * DSA_DH, s), BF16),
        scratch_shapes=[pltpu.VMEM((s, tq), I32),
                        pltpu.VMEM((1, tq), I32),
                        pltpu.VMEM((DSA_HEADS, KV_RANK, tq), F32),
                        pltpu.VMEM((DSA_HEADS, 1, tq), F32),
                        pltpu.VMEM((DSA_HEADS, 1, tq), F32)],
        compiler_params=pltpu.CompilerParams(
            dimension_semantics=("parallel", "arbitrary"), vmem_limit_bytes=VMEM_LIMIT),
        name="dsa",
    )(bias_far, qidxt, widxt, kidx, qabst, ckv, ckvt, btab, wuvt)


def _memkv_kernel(mem_ref, g_ref, w_ref, o_ref):
    mn = (_rms(mem_ref[...]) * g_ref[...]).astype(BF16)
    o_ref[...] = _dot(mn, w_ref[...]).astype(o_ref.dtype)


def _memkv(mem, g, w):
    b, m, d = mem.shape
    return pl.pallas_call(
        _memkv_kernel,
        grid=(b,),
        in_specs=[pl.BlockSpec((None, m, d), lambda bi: (bi, 0, 0)),
                  pl.BlockSpec(g.shape, lambda bi: (0, 0)),
                  pl.BlockSpec(w.shape, lambda bi: (0, 0))],
        out_specs=pl.BlockSpec((None, m, w.shape[1]), lambda bi: (bi, 0, 0)),
        out_shape=jax.ShapeDtypeStruct((b, m, w.shape[1]), BF16),
        compiler_params=pltpu.CompilerParams(
            dimension_semantics=("parallel",), vmem_limit_bytes=VMEM_LIMIT),
        name="memkv",
    )(mem, g, w)


def _mix_kernel(x_ref, odn_ref, odsat_ref, wout_ref, gx_ref, wxq_ref, kv_ref, wxo_ref, o_ref):
    d = x_ref.shape[-1]
    dh = d // X_HEADS
    n_dn = odn_ref.shape[-1]
    x1 = x_ref[...] + _dot(odn_ref[...], wout_ref[0:n_dn, :]) \
        + _dot_tn(odsat_ref[...], wout_ref[n_dn:, :])
    hx = (_rms(x1) * gx_ref[...]).astype(BF16)
    qx = _dot(hx, wxq_ref[...]).astype(BF16)
    parts = []
    for h in range(X_HEADS):
        kh = kv_ref[:, h * dh:(h + 1) * dh]
        vh = kv_ref[:, d + h * dh:d + (h + 1) * dh]
        lg = _dot_nt(qx[:, h * dh:(h + 1) * dh], kh) * (dh ** -0.5)
        mx = jnp.max(lg, axis=-1, keepdims=True)
        p = jnp.exp(lg - mx)
        p = p / jnp.sum(p, axis=-1, keepdims=True)
        parts.append(_dot(p.astype(BF16), vh).astype(BF16))
    ox = jnp.concatenate(parts, axis=-1)
    o_ref[...] = x1 + _dot(ox, wxo_ref[...])


def _mix(x, odn, odsat, wout, gx, wxq, kv, wxo):
    b, s, d = x.shape
    tm = TOK_TILE
    full = lambda a: pl.BlockSpec(a.shape, lambda bi, i: (0,) * a.ndim)
    return pl.pallas_call(
        _mix_kernel,
        grid=(b, s // tm),
        in_specs=[pl.BlockSpec((None, tm, d), lambda bi, i: (bi, i, 0)),
                  pl.BlockSpec((None, tm, odn.shape[-1]), lambda bi, i: (bi, i, 0)),
                  pl.BlockSpec((None, odsat.shape[1], tm), lambda bi, i: (bi, 0, i)),
                  full(wout), full(gx), full(wxq),
                  pl.BlockSpec((None,) + kv.shape[1:], lambda bi, i: (bi, 0, 0)),
                  full(wxo)],
        out_specs=pl.BlockSpec((None, tm, d), lambda bi, i: (bi, i, 0)),
        out_shape=jax.ShapeDtypeStruct((b, s, d), F32),
        compiler_params=pltpu.CompilerParams(
            dimension_semantics=("parallel", "parallel"), vmem_limit_bytes=VMEM_LIMIT),
        name="mix",
    )(x, odn, odsat, wout, gx, wxq, kv, wxo)


def _ffn_kernel(x_ref, g_ref, wg_ref, wu_ref, wd_ref, gf_ref, o_ref, hf_ref, acc_ref, *, final_norm):
    j = pl.program_id(2)

    @pl.when(j == 0)
    def _init():
        hf_ref[...] = (_rms(x_ref[...]) * g_ref[...]).astype(BF16)
        acc_ref[...] = jnp.zeros_like(acc_ref)

    hf = hf_ref[...]
    gate = _dot(hf, wg_ref[...])
    up = _dot(hf, wu_ref[...])
    act = (gate * _sigmoid(gate) * up).astype(BF16)
    acc_ref[...] += _dot(act, wd_ref[...])

    @pl.when(j == pl.num_programs(2) - 1)
    def _done():
        y = x_ref[...] + acc_ref[...]
        if final_norm:
            y = _rms(y) * gf_ref[...]
        o_ref[...] = y


def _ffn(x, g, wg, wu, wd, gf, final_norm):
    b, s, d = x.shape
    tm = TOK_TILE
    d_ff = wg.shape[1]
    n_ff = 2
    tf = d_ff // n_ff
    assert tf * n_ff == d_ff and tf % LANE == 0
    return pl.pallas_call(
        functools.partial(_ffn_kernel, final_norm=final_norm),
        grid=(b, s // tm, n_ff),
        in_specs=[pl.BlockSpec((None, tm, d), lambda bi, i, j: (bi, i, 0)),
                  pl.BlockSpec(g.shape, lambda bi, i, j: (0, 0)),
                  pl.BlockSpec((d, tf), lambda bi, i, j: (0, j)),
                  pl.BlockSpec((d, tf), lambda bi, i, j: (0, j)),
                  pl.BlockSpec((tf, d), lambda bi, i, j: (j, 0)),
                  pl.BlockSpec(gf.shape, lambda bi, i, j: (0, 0))],
        out_specs=pl.BlockSpec((None, tm, d), lambda bi, i, j: (bi, i, 0)),
        out_shape=jax.ShapeDtypeStruct((b, s, d), F32),
        scratch_shapes=[pltpu.VMEM((tm, d), BF16), pltpu.VMEM((tm, d), F32)],
        compiler_params=pltpu.CompilerParams(
            dimension_semantics=("parallel", "parallel", "arbitrary"), vmem_limit_bytes=VMEM_LIMIT),
        name="ffn",
    )(x, g, wg, wu, wd, gf)


def _relative_bucket(dist):
    max_exact = NUM_BUCKETS // 2
    d = jnp.maximum(dist, 0)
    log_ratio = jnp.log(jnp.maximum(d, max_exact).astype(F32) / max_exact) / math.log(MAX_DISTANCE / max_exact)
    large = max_exact + (log_ratio * (NUM_BUCKETS - max_exact)).astype(I32)
    return jnp.where(d < max_exact, d, jnp.minimum(large, NUM_BUCKETS - 1))


def _bias_tables(rel_bias):
    assert DSA_TQ == DSA_KB and DSA_TQ > MAX_DISTANCE
    kpos = np.arange(DSA_KB)[:, None]
    qpos = np.arange(DSA_TQ)[None, :]
    dist = jnp.asarray(np.stack([qpos - kpos, DSA_TQ + qpos - kpos]), I32)
    tab = rel_bias[_relative_bucket(dist)]
    far = rel_bias[_relative_bucket(jnp.asarray(2 * DSA_TQ, I32))]
    return jnp.transpose(tab, (0, 3, 1, 2)).astype(F32), far.astype(F32)


def _row(v, width=None, offset=0):
    v = v.astype(F32).reshape(1, -1)
    if width is None:
        return v
    return jnp.zeros((1, width), F32).at[:, offset:offset + v.shape[1]].set(v)


def kernel(x, mem, g_mix, w_in, conv_w, a_log, dt_bias, dn_norm_g, q_norm_g, kv_norm_g, w_uq, w_uk, w_uv,
           w_qidx, kidx_ln_g, kidx_ln_b, rel_bias, w_out, g_xattn, g_mem, w_xq, w_xkv, w_xo, g_ffn,
           w_gate, w_up, w_down, g_final):
    depth = w_in.shape[0]
    n_qk = DN_HEADS * DN_D
    o_db = 4 * n_qk
    o_da = o_db + DN_HEADS
    o_cq = o_da + DN_HEADS
    o_ckv = o_cq + Q_RANK
    o_kix = o_ckv + KV_RANK
    o_wix = o_kix + IDX_DIM
    o_end = o_wix + IDX_HEADS
    assert w_in.shape[2] == o_end
    btab, bias_far = _bias_tables(rel_bias)
    eye = jnp.eye(KV_RANK, dtype=BF16)
    for l in range(depth):
        wi = w_in[l]
        pad = jnp.zeros((wi.shape[0], SM_W - (IDX_DIM + IDX_HEADS + 2 * DN_HEADS)), wi.dtype)
        w1 = jnp.concatenate([wi[:, :o_db], wi[:, o_cq:o_ckv], wi[:, o_ckv:o_kix], wi[:, o_kix:o_wix],
                              wi[:, o_wix:o_end], wi[:, o_db:o_da], wi[:, o_da:o_cq], pad], axis=1).astype(BF16)
        wwt = wi[:, o_wix:o_end].T.astype(BF16)
        wuqt = w_uq[l].T.astype(BF16)
        wukt = jnp.transpose(w_uk[l], (0, 2, 1)).astype(BF16)
        wqit = w_qidx[l].T.astype(BF16)
        wuvt = jnp.transpose(w_uv[l], (0, 2, 1)).astype(BF16)
        qkv, dz, small, kidx, widxt, ckv, ckvt, qidxt, qabst = _inproj(
            x, _row(g_mix[l]), w1, wwt, _row(q_norm_g[l]), _row(kv_norm_g[l]),
            _row(kidx_ln_g[l], SM_W, SM_KIX), _row(kidx_ln_b[l], SM_W, SM_KIX), wuqt, wukt, wqit, eye)
        o_dn = _gdn(qkv, conv_w[l].astype(F32), small, _row(a_log[l], SM_W, SM_DA),
                    _row(dt_bias[l], SM_W, SM_DA), dz, _row(dn_norm_g[l]))
        o_dsat = _dsa(bias_far, qidxt, widxt, kidx, qabst, ckv, ckvt, btab, wuvt)
        kv = _memkv(mem, _row(g_mem[l]), w_xkv[l].astype(BF16))
        x = _mix(x, o_dn, o_dsat, w_out[l].astype(BF16), _row(g_xattn[l]), w_xq[l].astype(BF16), kv,
                 w_xo[l].astype(BF16))
        x = _ffn(x, _row(g_ffn[l]), w_gate[l].astype(BF16), w_up[l].astype(BF16), w_down[l].astype(BF16),
                 _row(g_final), final_norm=(l == depth - 1))
    return x
```
